```python
import jax
import jax.numpy as jnp
from jax import lax
import numpy as np

D_MODEL = 2048
BATCH = 16
SEQ = 2048
DEPTH = 4

CHUNK = 64
N_MIXERS = 3
N_HGRN = (DEPTH + 2) // 3
N_ATTN = (DEPTH + 1) // 3
N_GDN = DEPTH // 3

DEEPNORM_ALPHA = (2.0 * DEPTH) ** 0.25
DEEPNORM_BETA = (8.0 * DEPTH) ** -0.25
LN_EPS = 1e-5
RMS_EPS = 1e-6

HG_HEAD_DIM = 128
HG_HEADS = D_MODEL // HG_HEAD_DIM

AT_HEAD_DIM = 64
AT_HEADS = D_MODEL // AT_HEAD_DIM
AT_LEFT_CHUNKS = 8
AT_BAND = (AT_LEFT_CHUNKS + 1) * CHUNK
AT_MAX_REL = 256
AT_N_REL = CHUNK + AT_MAX_REL

GD_HEAD_DIM = 128
GD_K_HEADS = D_MODEL // GD_HEAD_DIM
GD_V_HEADS = 2 * GD_K_HEADS
GD_KEY_DIM = GD_K_HEADS * GD_HEAD_DIM
GD_VAL_DIM = GD_V_HEADS * GD_HEAD_DIM
GD_CONV = 4
GD_CONV_DIM = 2 * GD_KEY_DIM + GD_VAL_DIM
GD_IN = GD_CONV_DIM + GD_VAL_DIM + 2 * GD_V_HEADS

N_EXPERTS = 64
TOP_K = 8
N_GROUPS = 8
TOPK_GROUPS = 4
D_EXPERT = 384
D_SHARED = 384
ROUTE_SCALE = 2.5
MOE_BLOCK = 256

kernel_name = 'hybrid_chunk_causal_encoder'


def layer_norm(x, g, b):
    xf = x.astype(jnp.float32)
    mu = jnp.mean(xf, axis=-1, keepdims=True)
    var = jnp.mean(jnp.square(xf - mu), axis=-1, keepdims=True)
    y = (xf - mu) * lax.rsqrt(var + LN_EPS) * g.astype(jnp.float32) + b.astype(jnp.float32)
    return y.astype(x.dtype)


def rms_norm(x, w):
    xf = x.astype(jnp.float32)
    return xf * lax.rsqrt(jnp.mean(xf * xf, axis=-1, keepdims=True) + RMS_EPS) * w.astype(jnp.float32)


def l2_norm(x):
    xf = x.astype(jnp.float32)
    return xf * lax.rsqrt(jnp.sum(xf * xf, axis=-1, keepdims=True) + RMS_EPS)


def to_chunks(t):
    b, s, h, d = t.shape
    return t.reshape(b, s // CHUNK, CHUNK, h, d).transpose(0, 3, 1, 2, 4)


def scalar_to_chunks(t):
    b, s, h = t.shape
    return t.reshape(b, s // CHUNK, CHUNK, h).transpose(0, 3, 1, 2)


def from_chunks(t):
    b, h, n, c, d = t.shape
    return t.transpose(0, 2, 3, 1, 4).reshape(b, n * c, h * d)


def chunk_gla(q, k, v, log_f):
    causal = jnp.tril(jnp.ones((CHUNK, CHUNK), dtype=bool))
    G = jnp.cumsum(log_f, axis=3)
    G_mid = G[:, :, :, CHUNK // 2 - 1:CHUNK // 2]
    qg = q * jnp.exp(G - G_mid)
    kg = k * jnp.exp(G_mid - G)
    att = jnp.where(causal, jnp.einsum('bhncd,bhnsd->bhncs', qg, kg), 0.0)
    o_intra = jnp.einsum('bhncs,bhnsv->bhncv', att, v)
    G_last = G[:, :, :, -1:]
    q_inter = q * jnp.exp(G)
    k_state = k * jnp.exp(G_last - G)
    decay_last = jnp.exp(G_last[:, :, :, 0])

    def step(S, inp):
        qi, ki, vi, di = inp
        o = jnp.einsum('bhcd,bhdv->bhcv', qi, S)
        S = di[..., None] * S + jnp.einsum('bhcd,bhcv->bhdv', ki, vi)
        return S, o

    S0 = jnp.zeros(q.shape[:2] + (q.shape[-1], v.shape[-1]), jnp.float32)
    mv = lambda t: jnp.moveaxis(t, 2, 0)
    _, o_inter = lax.scan(step, S0, (mv(q_inter), mv(k_state), mv(v), mv(decay_last)))
    return o_intra + jnp.moveaxis(o_inter, 0, 2)


def hgrn2_mixer(x, w_in, lower_bound, norm_w, w_out):
    b, s, _ = x.shape
    hd = (b, s, HG_HEADS, HG_HEAD_DIM)
    q, f, i, g = jnp.split(x @ w_in, 4, axis=-1)
    q = jax.nn.silu(q.astype(jnp.float32)).reshape(hd) * HG_HEAD_DIM ** -0.5
    forget = lower_bound + (1.0 - lower_bound) * jax.nn.sigmoid(f.astype(jnp.float32))
    log_f = jnp.log(forget).reshape(hd)
    k = (1.0 - forget).reshape(hd)
    v = i.astype(jnp.float32).reshape(hd)
    o = chunk_gla(to_chunks(q), to_chunks(k), to_chunks(v), to_chunks(log_f))
    o = rms_norm(from_chunks(o).reshape(hd), norm_w).reshape(b, s, D_MODEL)
    o = o * jax.nn.silu(g.astype(jnp.float32))
    return o.astype(x.dtype) @ w_out


def chunk_attention_mixer(x, w_in, rel_bias, w_out):
    b, s, _ = x.shape
    n = s // CHUNK
    hd = (b, s, AT_HEADS, AT_HEAD_DIM)
    q, k, v = jnp.split(x @ w_in, 3, axis=-1)
    q = q.reshape(hd) * AT_HEAD_DIM ** -0.5
    pad = ((0, 0), (AT_LEFT_CHUNKS * CHUNK, 0), (0, 0), (0, 0))
    kp = jnp.pad(k.reshape(hd), pad)
    vp = jnp.pad(v.reshape(hd), pad)
    qi = jnp.arange(CHUNK)[:, None]
    kj = jnp.arange(AT_BAND)[None, :]
    dist = qi + AT_LEFT_CHUNKS * CHUNK - kj
    rel_idx = jnp.clip(dist, -(CHUNK - 1), AT_MAX_REL) + (CHUNK - 1)
    bias = rel_bias.astype(jnp.float32)[:, rel_idx]
    qc = jnp.moveaxis(q.reshape(b, n, CHUNK, AT_HEADS, AT_HEAD_DIM), 1, 0)

    def one_chunk(args):
        c, q_c = args
        start = c * CHUNK
        k_band = lax.dynamic_slice_in_dim(kp, start, AT_BAND, axis=1)
        v_band = lax.dynamic_slice_in_dim(vp, start, AT_BAND, axis=1)
        scores = jnp.einsum('bqhd,bkhd->bhqk', q_c, k_band).astype(jnp.float32) + bias
        key_pos = (c - AT_LEFT_CHUNKS) * CHUNK + jnp.arange(AT_BAND)
        scores = jnp.where(key_pos >= 0, scores, -jnp.inf)
        p = jax.nn.softmax(scores, axis=-1)
        return jnp.einsum('bhqk,bkhd->bqhd', p.astype(v_band.dtype), v_band)

    out = lax.map(one_chunk, (jnp.arange(n), qc))
    out = jnp.moveaxis(out, 0, 1).reshape(b, s, D_MODEL)
    return out @ w_out


def causal_depthwise_conv(x, w):
    ch = x.shape[-1]
    return lax.conv_general_dilated(
        x, w[:, None, :].astype(x.dtype), window_strides=(1,),
        padding=[(GD_CONV - 1, 0)], dimension_numbers=('NWC', 'WIO', 'NWC'),
        feature_group_count=ch)


def chunk_gated_delta_rule(q, k, v, beta, g):
    lower = jnp.tril(jnp.ones((CHUNK, CHUNK), dtype=bool))
    strict = jnp.tril(jnp.ones((CHUNK, CHUNK), dtype=bool), -1)
    G = jnp.cumsum(g, axis=-1)
    diff = G[..., :, None] - G[..., None, :]
    decay = jnp.where(lower, jnp.exp(jnp.where(lower, diff, 0.0)), 0.0)
    kk = jnp.einsum('bhncd,bhnsd->bhncs', k, k)
    L = jnp.where(strict, beta[..., None] * kk * decay, 0.0)
    eye = jnp.eye(CHUNK, dtype=jnp.float32)
    T = lax.linalg.triangular_solve(eye + L, jnp.broadcast_to(eye, L.shape), left_side=True, lower=True)
    u = jnp.einsum('bhncs,bhnsv->bhncv', T, v * beta[..., None])
    w = jnp.einsum('bhncs,bhnsd->bhncd', T, k * (beta * jnp.exp(G))[..., None])
    att = jnp.where(lower, jnp.einsum('bhncd,bhnsd->bhncs', q, k) * decay, 0.0)
    q_inter = q * jnp.exp(G)[..., None]
    k_state = k * jnp.exp(G[..., -1:] - G)[..., None]
    decay_last = jnp.exp(G[..., -1])

    def step(S, inp):
        att_i, q_i, k_i, u_i, w_i, d_i = inp
        v_new = u_i - jnp.einsum('bhcd,bhdv->bhcv', w_i, S)
        o = jnp.einsum('bhcd,bhdv->bhcv', q_i, S) + jnp.einsum('bhcs,bhsv->bhcv', att_i, v_new)
        S = d_i[..., None, None] * S + jnp.einsum('bhcd,bhcv->bhdv', k_i, v_new)
        return S, o

    S0 = jnp.zeros(q.shape[:2] + (q.shape[-1], v.shape[-1]), jnp.float32)
    mv = lambda t: jnp.moveaxis(t, 2, 0)
    _, o = lax.scan(step, S0, (mv(att), mv(q_inter), mv(k_state), mv(u), mv(w), mv(decay_last)))
    return jnp.moveaxis(o, 0, 2)


def gated_deltanet_mixer(x, w_in, conv_w, a_log, dt_bias, norm_w, w_out):
    b, s, _ = x.shape
    splits = [GD_CONV_DIM, GD_CONV_DIM + GD_VAL_DIM, GD_CONV_DIM + GD_VAL_DIM + GD_V_HEADS]
    qkv, z, beta_logit, a = jnp.split(x @ w_in, splits, axis=-1)
    qkv = jax.nn.silu(causal_depthwise_conv(qkv, conv_w))
    q, k, v = jnp.split(qkv, [GD_KEY_DIM, 2 * GD_KEY_DIM], axis=-1)
    rep = GD_V_HEADS // GD_K_HEADS
    khd = (b, s, GD_K_HEADS, GD_HEAD_DIM)
    vhd = (b, s, GD_V_HEADS, GD_HEAD_DIM)
    q = jnp.repeat(l2_norm(q.reshape(khd)), rep, axis=2) * GD_HEAD_DIM ** -0.5
    k = jnp.repeat(l2_norm(k.reshape(khd)), rep, axis=2)
    v = v.reshape(vhd).astype(jnp.float32)
    beta = jax.nn.sigmoid(beta_logit.astype(jnp.float32))
    g = -jnp.exp(a_log.astype(jnp.float32)) * jax.nn.softplus(
        a.astype(jnp.float32) + dt_bias.astype(jnp.float32))
    o = chunk_gated_delta_rule(to_chunks(q), to_chunks(k), to_chunks(v),
                               scalar_to_chunks(beta), scalar_to_chunks(g))
    o = rms_norm(from_chunks(o).reshape(vhd), norm_w)
    o = o * jax.nn.silu(z.astype(jnp.float32)).reshape(vhd)
    return o.reshape(b, s, GD_VAL_DIM).astype(x.dtype) @ w_out


def swiglu(h, w_gate_up, w_down):
    gate, up = jnp.split(h @ w_gate_up, 2, axis=-1)
    return (jax.nn.silu(gate) * up) @ w_down


def moe_ffn(x, w_router, router_bias, w_gate_up, w_down, ws_gate_up, ws_down):
    b, s, d = x.shape
    xt = x.reshape(-1, d)
    n_tok = xt.shape[0]
    scores = jax.nn.sigmoid(xt.astype(jnp.float32) @ w_router.astype(jnp.float32))
    sel = scores + router_bias.astype(jnp.float32)
    grp = sel.reshape(n_tok, N_GROUPS, N_EXPERTS // N_GROUPS)
    grp_score = lax.top_k(grp, 2)[0].sum(-1)
    _, top_grp = lax.top_k(grp_score, TOPK_GROUPS)
    grp_mask = (top_grp[:, :, None] == jnp.arange(N_GROUPS)).any(axis=1)
    sel = jnp.where(jnp.repeat(grp_mask, N_EXPERTS // N_GROUPS, axis=1), sel, -jnp.inf)
    _, top_e = lax.top_k(sel, TOP_K)
    gate = jnp.take_along_axis(scores, top_e, axis=-1)
    gate = gate / jnp.sum(gate, axis=-1, keepdims=True) * ROUTE_SCALE

    n_assign = n_tok * TOP_K
    flat_e = top_e.reshape(-1)
    flat_tok = jnp.arange(n_assign, dtype=jnp.int32) // TOP_K
    flat_w = gate.reshape(-1)
    order = jnp.argsort(flat_e)
    se, st, sw = flat_e[order], flat_tok[order], flat_w[order]
    counts = jnp.bincount(flat_e, length=N_EXPERTS)
    padded = (counts + MOE_BLOCK - 1) // MOE_BLOCK * MOE_BLOCK
    pad_end = jnp.cumsum(padded)
    pad_start = pad_end - padded
    raw_start = jnp.cumsum(counts) - counts
    dest = pad_start[se] + jnp.arange(n_assign, dtype=jnp.int32) - raw_start[se]
    n_blocks = -(-n_assign // MOE_BLOCK) + N_EXPERTS
    n_rows = n_blocks * MOE_BLOCK
    row_tok = jnp.zeros((n_rows,), jnp.int32).at[dest].set(st)
    row_w = jnp.zeros((n_rows,), jnp.float32).at[dest].set(sw)
    block_e = jnp.minimum(
        jnp.searchsorted(pad_end, jnp.arange(n_blocks) * MOE_BLOCK, side='right'), N_EXPERTS - 1)

    shared = swiglu(xt, ws_gate_up, ws_down)

    def body(acc, blk):
        tok, wt, e = blk
        y = swiglu(xt[tok], w_gate_up[e], w_down[e])
        return acc.at[tok].add((y * wt[:, None]).astype(acc.dtype)), None

    out, _ = lax.scan(body, shared, (row_tok.reshape(n_blocks, MOE_BLOCK),
                                     row_w.reshape(n_blocks, MOE_BLOCK), block_e))
    return out.reshape(b, s, d)


def setup_inputs(seed: int = 0) -> dict:
    key = jax.random.key(seed)
    ks = iter(jax.random.split(key, 40))
    D = D_MODEL
    beta = DEEPNORM_BETA

    def nrm(shape, scale):
        return jax.random.normal(next(ks), shape, jnp.float32) * scale

    def gain(shape):
        return 1.0 + nrm(shape, 0.02)

    return {
        'x': nrm((BATCH, SEQ, D), 1.0),
        'hg_w_in': nrm((N_HGRN, D, 4 * D), D ** -0.5),
        'hg_lower_bounds': nrm((N_HGRN, D), 0.1),
        'hg_norm_w': gain((N_HGRN, HG_HEAD_DIM)),
        'hg_w_out': nrm((N_HGRN, D, D), beta * D ** -0.5),
        'at_w_in': nrm((N_ATTN, D, 3 * D), D ** -0.5),
        'at_rel_bias': nrm((N_ATTN, AT_HEADS, AT_N_REL), 0.1),
        'at_w_out': nrm((N_ATTN, D, D), beta * D ** -0.5),
        'gd_w_in': nrm((N_GDN, D, GD_IN), D ** -0.5),
        'gd_conv_w': nrm((N_GDN, GD_CONV, GD_CONV_DIM), GD_CONV ** -0.5),
        'gd_a_log': jnp.log(jax.random.uniform(next(ks), (N_GDN, GD_V_HEADS), jnp.float32, 1.0, 16.0)),
        'gd_dt_bias': nrm((N_GDN, GD_V_HEADS), 0.1),
        'gd_norm_w': gain((N_GDN, GD_HEAD_DIM)),
        'gd_w_out': nrm((N_GDN, GD_VAL_DIM, D), beta * GD_VAL_DIM ** -0.5),
        'ln1_g': gain((DEPTH, D)),
        'ln1_b': nrm((DEPTH, D), 0.02),
        'ln2_g': gain((DEPTH, D)),
        'ln2_b': nrm((DEPTH, D), 0.02),
        'moe_w_router': nrm((DEPTH, D, N_EXPERTS), D ** -0.5),
        'moe_router_bias': nrm((DEPTH, N_EXPERTS), 0.01),
        'moe_w_gate_up': nrm((DEPTH, N_EXPERTS, D, 2 * D_EXPERT), D ** -0.5),
        'moe_w_down': nrm((DEPTH, N_EXPERTS, D_EXPERT, D), beta * D_EXPERT ** -0.5),
        'moe_ws_gate_up': nrm((DEPTH, D, 2 * D_SHARED), D ** -0.5),
        'moe_ws_down': nrm((DEPTH, D_SHARED, D), beta * D_SHARED ** -0.5),
    }


def reference(x, hg_w_in, hg_lower_bounds, hg_norm_w, hg_w_out,
              at_w_in, at_rel_bias, at_w_out,
              gd_w_in, gd_conv_w, gd_a_log, gd_dt_bias, gd_norm_w, gd_w_out,
              ln1_g, ln1_b, ln2_g, ln2_b,
              moe_w_router, moe_router_bias, moe_w_gate_up, moe_w_down,
              moe_ws_gate_up, moe_ws_down):
    lb_soft = jax.nn.softmax(hg_lower_bounds.astype(jnp.float32), axis=0)
    lower_bounds = jnp.cumsum(lb_soft, axis=0) - lb_soft[0]
    for i in range(DEPTH):
        j = i // N_MIXERS
        kind = i % N_MIXERS
        if kind == 0:
            h = hgrn2_mixer(x, hg_w_in[j], lower_bounds[j], hg_norm_w[j], hg_w_out[j])
        elif kind == 1:
            h = chunk_attention_mixer(x, at_w_in[j], at_rel_bias[j], at_w_out[j])
        else:
            h = gated_deltanet_mixer(x, gd_w_in[j], gd_conv_w[j], gd_a_log[j], gd_dt_bias[j],
                                     gd_norm_w[j], gd_w_out[j])
        x = layer_norm(DEEPNORM_ALPHA * x + h, ln1_g[i], ln1_b[i])
        m = moe_ffn(x, moe_w_router[i], moe_router_bias[i], moe_w_gate_up[i], moe_w_down[i],
                    moe_ws_gate_up[i], moe_ws_down[i])
        x = layer_norm(DEEPNORM_ALPHA * x + m, ln2_g[i], ln2_b[i])
    return x
```

```python
import functools

import jax
import jax.numpy as jnp
from jax import lax
from jax.experimental import pallas as pl
from jax.experimental.pallas import tpu as pltpu

F32 = jnp.float32
BF16 = jnp.bfloat16
I32 = jnp.int32

CHUNK = 64
LANES = 128
LN_EPS = 1e-5
RMS_EPS = 1e-6
AT_HEAD_DIM = 64
AT_LEFT_CHUNKS = 8
AT_MAX_REL = 256
GD_CONV = 4
N_GROUPS = 8
TOPK_GROUPS = 4
TOP_K = 8
ROUTE_SCALE = 2.5
VMEM_LIMIT = 56 * 1024 * 1024


def _cparams(sem):
    return pltpu.CompilerParams(dimension_semantics=sem, vmem_limit_bytes=VMEM_LIMIT)


def _dot(a, b):
    return jnp.dot(a, b, preferred_element_type=F32)


def _dot_nt(a, b):
    return lax.dot_general(a, b, (((1,), (1,)), ((), ())), preferred_element_type=F32)


def _dot_tn(a, b):
    return lax.dot_general(a, b, (((0,), (0,)), ((), ())), preferred_element_type=F32)


def _split2(x):
    hi = x.astype(BF16)
    lo = (x - hi.astype(F32)).astype(BF16)
    return hi, lo


def _split3(x):
    hi = x.astype(BF16)
    r = x - hi.astype(F32)
    mid = r.astype(BF16)
    lo = (r - mid.astype(F32)).astype(BF16)
    return hi, mid, lo


def _dot3(a, b, dot=_dot):
    ah, al = _split2(a)
    bh, bl = _split2(b)
    return dot(ah, bh) + (dot(ah, bl) + dot(al, bh))


def _dot_exact_lhs(mask_bf16, x, dot=_dot):
    h, m, l = _split3(x)
    return dot(mask_bf16, h) + (dot(mask_bf16, m) + dot(mask_bf16, l))


def _sigmoid(x):
    return 1.0 / (1.0 + jnp.exp(-x))


def _silu(x):
    return x * _sigmoid(x)


def _softplus(x):
    return jnp.maximum(x, 0.0) + jnp.log(1.0 + jnp.exp(-jnp.abs(x)))


def _mm_kernel(x_ref, w_ref, o_ref):
    o_ref[...] = _dot(x_ref[...], w_ref[...]).astype(o_ref.dtype)


def _matmul(x, w, *, tm, tn, out_dtype=F32):
    t, k = x.shape
    n = w.shape[1]
    tm, tn = min(tm, t), min(tn, n)
    return pl.pallas_call(
        _mm_kernel,
        grid=(t // tm, n // tn),
        in_specs=[pl.BlockSpec((tm, k), lambda i, j: (i, 0)),
                  pl.BlockSpec((k, tn), lambda i, j: (0, j))],
        out_specs=pl.BlockSpec((tm, tn), lambda i, j: (i, j)),
        out_shape=jax.ShapeDtypeStruct((t, n), out_dtype),
        compiler_params=_cparams(("parallel", "arbitrary")),
        name="proj_matmul",
    )(x, w)


def _layer_norm_rows(y, g, b):
    mu = jnp.mean(y, axis=-1, keepdims=True)
    d = y - mu
    var = jnp.mean(d * d, axis=-1, keepdims=True)
    return d * lax.rsqrt(var + LN_EPS) * g + b


def _mm_ln_kernel(h_ref, w_ref, x_ref, g_ref, b_ref, o_ref, ob_ref, *, alpha):
    y = alpha * x_ref[...] + _dot(h_ref[...], w_ref[...])
    out = _layer_norm_rows(y, g_ref[...], b_ref[...])
    o_ref[...] = out
    ob_ref[...] = out.astype(BF16)


def _matmul_residual_ln(h, w, x, g, b, *, alpha, tm):
    t, k = h.shape
    d = w.shape[1]
    tm = min(tm, t)
    row = lambda i: (i, 0)
    fixed = lambda i: (0, 0)
    return pl.pallas_call(
        functools.partial(_mm_ln_kernel, alpha=alpha),
        grid=(t // tm,),
        in_specs=[pl.BlockSpec((tm, k), row), pl.BlockSpec((k, d), fixed),
                  pl.BlockSpec((tm, d), row), pl.BlockSpec((1, d), fixed), pl.BlockSpec((1, d), fixed)],
        out_specs=[pl.BlockSpec((tm, d), row), pl.BlockSpec((tm, d), row)],
        out_shape=[jax.ShapeDtypeStruct((t, d), F32), jax.ShapeDtypeStruct((t, d), BF16)],
        compiler_params=_cparams(("parallel",)),
        name="out_proj_ln",
    )(h, w, x, g.reshape(1, d), b.reshape(1, d))


def _hgrn_kernel(q_ref, f_ref, i_ref, g_ref, lbr_ref, nw_ref, o_ref, st_ref, *, layer_j, seq):
    c_len = CHUNK
    hd = q_ref.shape[1]
    lbr = lbr_ref[...]
    e = jnp.exp(lbr - jnp.max(lbr, axis=0, keepdims=True))
    soft = e / jnp.sum(e, axis=0, keepdims=True)
    lb = jnp.zeros((1, hd), F32)
    for r in range(1, layer_j + 1):
        lb = lb + soft[r:r + 1]
    nw = nw_ref[...]
    row = lax.broadcasted_iota(I32, (c_len, c_len), 0)
    col = lax.broadcasted_iota(I32, (c_len, c_len), 1)
    tril = row >= col
    tril_b = tril.astype(BF16)
    scale = hd ** -0.5
    st_ref[...] = jnp.zeros_like(st_ref)

    def body(c, carry):
        sl = pl.ds(pl.multiple_of(c * c_len, c_len), c_len)
        q = _silu(q_ref[sl, :]) * scale
        forget = lb + (1.0 - lb) * _sigmoid(f_ref[sl, :])
        logf = jnp.log(forget)
        k = 1.0 - forget
        vb = i_ref[sl, :].astype(BF16)
        gcum = _dot_exact_lhs(tril_b, logf)
        g_mid = gcum[c_len // 2 - 1:c_len // 2]
        g_last = gcum[c_len - 1:c_len]
        qg = (q * jnp.exp(gcum - g_mid)).astype(BF16)
        kg = (k * jnp.exp(g_mid - gcum)).astype(BF16)
        att = jnp.where(tril, _dot_nt(qg, kg), 0.0).astype(BF16)
        st = st_ref[...]
        o = _dot(att, vb) + _dot_nt((q * jnp.exp(gcum)).astype(BF16), st.astype(BF16))
        kst = (k * jnp.exp(g_last - gcum)).astype(BF16)
        st_ref[...] = st * jnp.exp(g_last) + _dot_tn(vb, kst)
        on = o * lax.rsqrt(jnp.mean(o * o, axis=-1, keepdims=True) + RMS_EPS) * nw
        o_ref[sl, :] = (on * _silu(g_ref[sl, :])).astype(o_ref.dtype)
        return carry

    lax.fori_loop(0, seq // c_len, body, 0)


def _hgrn_core(proj, lb_raw, norm_w, *, layer_j, batch, seq, d_model):
    hd = norm_w.shape[0]
    heads = d_model // hd
    t = batch * seq
    blk = lambda off: pl.BlockSpec((seq, hd), lambda b, h, off=off: (b, off + h))
    return pl.pallas_call(
        functools.partial(_hgrn_kernel, layer_j=layer_j, seq=seq),
        grid=(batch, heads),
        in_specs=[blk(0), blk(heads), blk(2 * heads), blk(3 * heads),
                  pl.BlockSpec((lb_raw.shape[0], hd), lambda b, h: (0, h)),
                  pl.BlockSpec((1, hd), lambda b, h: (0, 0))],
        out_specs=pl.BlockSpec((seq, hd), lambda b, h: (b, h)),
        out_shape=jax.ShapeDtypeStruct((t, d_model), BF16),
        scratch_shapes=[pltpu.VMEM((hd, hd), F32)],
        compiler_params=_cparams(("parallel", "parallel")),
        name="hgrn_core",
    )(proj, proj, proj, proj, lb_raw, norm_w.reshape(1, hd))


def _attn_kernel(q_ref, k_ref, v_ref, bias_ref, o_ref, kp_ref, vp_ref, *, seq):
    c_len = CHUNK
    pad = AT_LEFT_CHUNKS * c_len
    band = pad + c_len
    width = q_ref.shape[1]
    heads_here = width // AT_HEAD_DIM
    zeros = jnp.zeros((pad, width), BF16)
    kp_ref[0:pad, :] = zeros
    vp_ref[0:pad, :] = zeros
    kp_ref[pad:pad + seq, :] = k_ref[...].astype(BF16)
    vp_ref[pad:pad + seq, :] = v_ref[...].astype(BF16)
    lane = lax.broadcasted_iota(I32, (c_len, width), 1)
    key_off = lax.broadcasted_iota(I32, (c_len, band), 1)
    scale = AT_HEAD_DIM ** -0.5

    def body(c, carry):
        start = pl.multiple_of(c * c_len, c_len)
        q = q_ref[pl.ds(start, c_len), :] * scale
        kb = kp_ref[pl.ds(start, band), :]
        vb = vp_ref[pl.ds(start, band), :]
        valid = key_off + (c - AT_LEFT_CHUNKS) * c_len >= 0
        out = jnp.zeros((c_len, width), F32)
        for h in range(heads_here):
            in_head = (lane >= h * AT_HEAD_DIM) & (lane < (h + 1) * AT_HEAD_DIM)
            qh = jnp.where(in_head, q, 0.0).astype(BF16)
            s = _dot_nt(qh, kb) + bias_ref[h]
            s = jnp.where(valid, s, -jnp.inf)
            p = jnp.exp(s - jnp.max(s, axis=-1, keepdims=True))
            oh = _dot(p.astype(BF16), vb) / jnp.sum(p, axis=-1, keepdims=True)
            out = jnp.where(in_head, oh, out)
        o_ref[pl.ds(start, c_len), :] = out.astype(o_ref.dtype)
        return carry

    lax.fori_loop(0, seq // c_len, body, 0)


def _attn_core(qkv, bias, *, batch, seq, d_model):
    t = batch * seq
    nblk = d_model // LANES
    heads_here = LANES // AT_HEAD_DIM
    band = (AT_LEFT_CHUNKS + 1) * CHUNK
    blk = lambda off: pl.BlockSpec((seq, LANES), lambda b, h, off=off: (b, off + h))
    return pl.pallas_call(
        functools.partial(_attn_kernel, seq=seq),
        grid=(batch, nblk),
        in_specs=[blk(0), blk(nblk), blk(2 * nblk),
                  pl.BlockSpec((heads_here, CHUNK, band), lambda b, h: (h, 0, 0))],
        out_specs=pl.BlockSpec((seq, LANES), lambda b, h: (b, h)),
        out_shape=jax.ShapeDtypeStruct((t, d_model), BF16),
        scratch_shapes=[pltpu.VMEM((seq + AT_LEFT_CHUNKS * CHUNK, LANES), BF16),
                        pltpu.VMEM((seq + AT_LEFT_CHUNKS * CHUNK, LANES), BF16)],
        compiler_params=_cparams(("parallel", "parallel")),
        name="attn_core",
    )(qkv, qkv, qkv, bias)


def _unit_lower_inverse(lm):
    n = lm.shape[0]
    row = lax.broadcasted_iota(I32, (n, n), 0)
    col = lax.broadcasted_iota(I32, (n, n), 1)
    x = -lm
    inv = jnp.where(row == col, 1.0, 0.0) + x
    p = x
    power = 2
    while power < n:
        p = _dot3(p, p)
        inv = inv + _dot3(inv, p)
        power *= 2
    return inv


def _gdn_kernel(q_ref, k_ref, v_ref, z_ref, cwq_ref, cwk_ref, cwv_ref, ba_ref, aT_ref, hp_ref, nw_ref,
                o_ref, qn_ref, kn_ref, vc_ref, u_ref, w_ref, att_ref, qi_ref, ks_ref, dl_ref, st_ref,
                *, seq, rep, v_heads, unroll_a):
    c_len = CHUNK
    hd = q_ref.shape[1]
    n_chunks = seq // c_len
    hk = pl.program_id(1)

    def conv_silu(x, w):
        rows = lax.broadcasted_iota(I32, x.shape, 0)
        y = x * w[GD_CONV - 1:GD_CONV]
        for s in range(1, GD_CONV):
            shifted = jnp.where(rows >= s, pltpu.roll(x, s, axis=0), 0.0)
            y = y + shifted * w[GD_CONV - 1 - s:GD_CONV - s]
        return _silu(y)

    q = conv_silu(q_ref[...], cwq_ref[...])
    qn_ref[...] = q * lax.rsqrt(jnp.sum(q * q, axis=-1, keepdims=True) + RMS_EPS) * (hd ** -0.5)
    k = conv_silu(k_ref[...], cwk_ref[...])
    kn_ref[...] = k * lax.rsqrt(jnp.sum(k * k, axis=-1, keepdims=True) + RMS_EPS)
    vc_ref[...] = conv_silu(v_ref[...], cwv_ref[...])

    row = lax.broadcasted_iota(I32, (c_len, c_len), 0)
    col = lax.broadcasted_iota(I32, (c_len, c_len), 1)
    lower = row >= col
    strict = row > col
    lower_b = lower.astype(BF16)
    upper_b = (row <= col).astype(BF16)
    lane = lax.broadcasted_iota(I32, (c_len, LANES), 1)
    nw = nw_ref[...]

    def prep_chunk(c):
        sl = pl.ds(pl.multiple_of(c * c_len, c_len), c_len)
        kc = kn_ref[sl, :]
        qc = qn_ref[sl, :]
        kcb = kc.astype(BF16)
        kk = _dot_nt(kcb, kcb)
        qk = _dot_nt(qc.astype(BF16), kcb)
        bac = ba_ref[sl, :]
        for j in range(rep):
            hv = rep * hk + j
            beta_col = _sigmoid(jnp.sum(jnp.where(lane == hv, bac, 0.0), axis=-1, keepdims=True))
            a_col = jnp.sum(jnp.where(lane == v_heads + hv, bac, 0.0), axis=-1, keepdims=True)
            a_log = hp_ref[j, 0:1, :]
            dt_b = hp_ref[j, 1:2, :]
            neg_a = -jnp.exp(a_log)
            g_col = neg_a * _softplus(a_col + dt_b)
            gcum_col = _dot_exact_lhs(lower_b, g_col)
            a_row = aT_ref[0, j, pl.ds(c, 1), :]
            g_row = neg_a[:, :c_len] * _softplus(a_row + dt_b[:, :c_len])
            gcum_row = _cumsum_row(g_row, upper_b)
            diff = gcum_col[:, :c_len] - gcum_row
            decay = jnp.where(lower, jnp.exp(jnp.where(lower, diff, 0.0)), 0.0)
            lm = jnp.where(strict, beta_col * kk * decay, 0.0)
            tinv = _unit_lower_inverse(lm).astype(BF16)
            vj = vc_ref[sl, j * hd:(j + 1) * hd]
            u_ref[j, sl, :] = _dot(tinv, (vj * beta_col).astype(BF16))
            w_ref[j, sl, :] = _dot(tinv, (kc * (beta_col * jnp.exp(gcum_col))).astype(BF16)).astype(BF16)
            att_ref[j, sl, :] = jnp.where(lower, qk * decay, 0.0).astype(BF16)
            qi_ref[j, sl, :] = (qc * jnp.exp(gcum_col)).astype(BF16)
            g_last = gcum_col[c_len - 1:c_len]
            ks_ref[j, sl, :] = (kc * jnp.exp(g_last - gcum_col)).astype(BF16)
            dl_ref[j, pl.ds(c, 1), :] = jnp.exp(g_last)

    def prep_body(cc, carry):
        for i in range(unroll_a):
            prep_chunk(cc * unroll_a + i)
        return carry

    lax.fori_loop(0, n_chunks // unroll_a, prep_body, 0)

    st_ref[...] = jnp.zeros_like(st_ref)

    def scan_body(c, carry):
        sl = pl.ds(pl.multiple_of(c * c_len, c_len), c_len)
        for j in range(rep):
            st = st_ref[j]
            stb = st.astype(BF16)
            v_new = u_ref[j, sl, :] - _dot(w_ref[j, sl, :], stb)
            v_newb = v_new.astype(BF16)
            o = _dot(qi_ref[j, sl, :], stb) + _dot(att_ref[j, sl, :], v_newb)
            st_ref[j] = dl_ref[j, pl.ds(c, 1), :] * st + _dot_tn(ks_ref[j, sl, :], v_newb)
            on = o * lax.rsqrt(jnp.mean(o * o, axis=-1, keepdims=True) + RMS_EPS) * nw
            o_ref[sl, j * hd:(j + 1) * hd] = (on * _silu(z_ref[sl, j * hd:(j + 1) * hd])).astype(o_ref.dtype)
        return carry

    lax.fori_loop(0, n_chunks, scan_body, 0)


def _cumsum_row(g_row, upper_b):
    c_len = g_row.shape[1]
    gb = jnp.broadcast_to(g_row, (8, c_len))
    h, m, l = _split3(gb)
    return (_dot(h, upper_b) + (_dot(m, upper_b) + _dot(l, upper_b)))[0:1]


def _gdn_core(qkvz, ba, ba_t, conv_w, head_params, norm_w, *, batch, seq, k_heads, v_heads):
    hd = norm_w.shape[0]
    rep = v_heads // k_heads
    t = batch * seq
    n_chunks = seq // CHUNK
    vw = rep * hd
    v_off = 2 * k_heads // rep
    return pl.pallas_call(
        functools.partial(_gdn_kernel, seq=seq, rep=rep, v_heads=v_heads, unroll_a=2),
        grid=(batch, k_heads),
        in_specs=[pl.BlockSpec((seq, hd), lambda b, h: (b, h)),
                  pl.BlockSpec((seq, hd), lambda b, h: (b, k_heads + h)),
                  pl.BlockSpec((seq, vw), lambda b, h: (b, v_off + h)),
                  pl.BlockSpec((seq, vw), lambda b, h: (b, v_off + k_heads + h)),
                  pl.BlockSpec((GD_CONV, hd), lambda b, h: (0, h)),
                  pl.BlockSpec((GD_CONV, hd), lambda b, h: (0, k_heads + h)),
                  pl.BlockSpec((GD_CONV, vw), lambda b, h: (0, v_off + h)),
                  pl.BlockSpec((seq, LANES), lambda b, h: (b, 0)),
                  pl.BlockSpec((1, rep, n_chunks, CHUNK), lambda b, h: (b, k_heads + h, 0, 0)),
                  pl.BlockSpec((rep, 8, LANES), lambda b, h: (h, 0, 0)),
                  pl.BlockSpec((1, hd), lambda b, h: (0, 0))],
        out_specs=pl.BlockSpec((seq, vw), lambda b, h: (b, h)),
        out_shape=jax.ShapeDtypeStruct((t, v_heads * hd), BF16),
        scratch_shapes=[pltpu.VMEM((seq, hd), F32), pltpu.VMEM((seq, hd), F32), pltpu.VMEM((seq, vw), F32),
                        pltpu.VMEM((rep, seq, hd), F32), pltpu.VMEM((rep, seq, hd), BF16),
                        pltpu.VMEM((rep, seq, CHUNK), BF16), pltpu.VMEM((rep, seq, hd), BF16),
                        pltpu.VMEM((rep, seq, hd), BF16), pltpu.VMEM((rep, n_chunks, LANES), F32),
                        pltpu.VMEM((rep, hd, hd), F32)],
        compiler_params=_cparams(("parallel", "parallel")),
        name="gdn_core",
    )(qkvz, qkvz, qkvz, qkvz, conv_w, conv_w, conv_w, ba, ba_t, head_params, norm_w.reshape(1, hd))


def _first_index_of_max(vals, idx, n):
    m = jnp.max(vals, axis=0, keepdims=True)
    first = jnp.min(jnp.where(vals == m, idx, n), axis=0, keepdims=True)
    return m, first


def _router_kernel(x_ref, wt_ref, rb_ref, te_ref, gt_ref, rk_ref, cnt_ref, carry_ref):
    n_exp = wt_ref.shape[0]
    tm = x_ref.shape[0]
    gsz = n_exp // N_GROUPS
    neg_inf = -jnp.inf

    @pl.when(pl.program_id(0) == 0)
    def _():
        carry_ref[...] = jnp.zeros_like(carry_ref)

    logits = _dot3(wt_ref[...], x_ref[...], dot=_dot_nt)
    scores = _sigmoid(logits)
    sel = scores + rb_ref[:, 0:1]

    gidx = lax.broadcasted_iota(I32, (gsz, tm), 0)
    grp_rows = []
    for g in range(N_GROUPS):
        blk = sel[g * gsz:(g + 1) * gsz]
        m1, first = _first_index_of_max(blk, gidx, gsz)
        m2 = jnp.max(jnp.where(gidx == first, neg_inf, blk), axis=0, keepdims=True)
        grp_rows.append(m1 + m2)
    grp = jnp.concatenate(grp_rows, axis=0)
    gi = lax.broadcasted_iota(I32, (N_GROUPS, tm), 0)
    keep = jnp.zeros((N_GROUPS, tm), F32)
    for _ in range(TOPK_GROUPS):
        _, first = _first_index_of_max(grp, gi, N_GROUPS)
        hit = gi == first
        keep = jnp.where(hit, 1.0, keep)
        grp = jnp.where(hit, neg_inf, grp)
    cur = jnp.concatenate(
        [jnp.where(keep[g:g + 1] > 0.0, sel[g * gsz:(g + 1) * gsz], neg_inf) for g in range(N_GROUPS)], axis=0)

    ei = lax.broadcasted_iota(I32, (n_exp, tm), 0)
    chosen = jnp.zeros((n_exp, tm), F32)
    top_e, gates = [], []
    for _ in range(TOP_K):
        _, first = _first_index_of_max(cur, ei, n_exp)
        hit = ei == first
        top_e.append(first)
        gates.append(jnp.sum(jnp.where(hit, scores, 0.0), axis=0, keepdims=True))
        chosen = jnp.where(hit, 1.0, chosen)
        cur = jnp.where(hit, neg_inf, cur)
    gate = jnp.concatenate(gates, axis=0)
    gate = gate / jnp.sum(gate, axis=0, keepdims=True) * ROUTE_SCALE

    r = lax.broadcasted_iota(I32, (tm, tm), 0)
    c = lax.broadcasted_iota(I32, (tm, tm), 1)
    incl = _dot(chosen.astype(BF16), (r <= c).astype(BF16))
    before = carry_ref[:, 0:1] + incl - chosen
    ranks = [jnp.sum(jnp.where(ei == top_e[k], before, 0.0), axis=0, keepdims=True) for k in range(TOP_K)]

    te_ref[...] = jnp.concatenate(top_e, axis=0)
    gt_ref[...] = gate
    rk_ref[...] = jnp.concatenate(ranks, axis=0).astype(I32)
    carry_ref[...] = carry_ref[...] + jnp.broadcast_to(incl[:, tm - 1:tm], carry_ref.shape)
    cnt_ref[...] = carry_ref[...]


def _route(x, w_router_t, router_bias, *, tm):
    t, d = x.shape
    n_exp = w_router_t.shape[0]
    tm = min(tm, t)
    tok = lambda i: (0, i)
    fixed = lambda i: (0, 0)
    rb = jnp.broadcast_to(router_bias.astype(F32)[:, None], (n_exp, LANES))
    return pl.pallas_call(
        _router_kernel,
        grid=(t // tm,),
        in_specs=[pl.BlockSpec((tm, d), lambda i: (i, 0)), pl.BlockSpec((n_exp, d), fixed),
                  pl.BlockSpec((n_exp, LANES), fixed)],
        out_specs=[pl.BlockSpec((TOP_K, tm), tok), pl.BlockSpec((TOP_K, tm), tok), pl.BlockSpec((TOP_K, tm), tok),
                   pl.BlockSpec((n_exp, LANES), fixed)],
        out_shape=[jax.ShapeDtypeStruct((TOP_K, t), I32), jax.ShapeDtypeStruct((TOP_K, t), F32),
                   jax.ShapeDtypeStruct((TOP_K, t), I32), jax.ShapeDtypeStruct((n_exp, LANES), F32)],
        scratch_shapes=[pltpu.VMEM((n_exp, LANES), F32)],
        compiler_params=_cparams(("arbitrary",)),
        name="moe_route",
    )(x, w_router_t, rb)


def _dest_kernel(te_ref, rk_ref, st_ref, d_ref):
    n_exp = st_ref.shape[0]
    tm = te_ref.shape[1]
    ei = lax.broadcasted_iota(I32, (n_exp, tm), 0)
    start = st_ref[:, 0:1]
    rows = [jnp.sum(jnp.where(ei == te_ref[k:k + 1, :], start, 0.0), axis=0, keepdims=True) for k in range(TOP_K)]
    d_ref[...] = jnp.concatenate(rows, axis=0).astype(I32) + rk_ref[...]


def _destinations(top_e_t, rank_t, starts, *, tm):
    t = top_e_t.shape[1]
    n_exp = starts.shape[0]
    tm = min(tm, t)
    tok = lambda i: (0, i)
    return pl.pallas_call(
        _dest_kernel,
        grid=(t // tm,),
        in_specs=[pl.BlockSpec((TOP_K, tm), tok), pl.BlockSpec((TOP_K, tm), tok),
                  pl.BlockSpec((n_exp, LANES), lambda i: (0, 0))],
        out_specs=pl.BlockSpec((TOP_K, tm), tok),
        out_shape=jax.ShapeDtypeStruct((TOP_K, t), I32),
        compiler_params=_cparams(("parallel",)),
        name="moe_dest",
    )(top_e_t, rank_t, starts)


def _scatter_kernel(dest_hbm, x_ref, init_hbm, xs_hbm, dest_smem, idx_sem, row_sem):
    del init_hbm
    ts = x_ref.shape[0]
    idx_copy = pltpu.make_async_copy(dest_hbm.at[pl.program_id(0)], dest_smem, idx_sem)
    idx_copy.start()
    idx_copy.wait()

    def row_copy(t, k):
        r = dest_smem[t * TOP_K + k]
        return pltpu.make_async_copy(x_ref.at[pl.ds(t, 1), :], xs_hbm.at[pl.ds(r, 1), :], row_sem)

    def start_body(t, carry):
        for k in range(TOP_K):
            row_copy(t, k).start()
        return carry

    def wait_body(t, carry):
        for k in range(TOP_K):
            row_copy(t, k).wait()
        return carry

    lax.fori_loop(0, ts, start_body, 0)
    lax.fori_loop(0, ts, wait_body, 0)


def _scatter_rows(dest_tiles, x, n_rows, *, ts):
    t, d = x.shape
    init = jnp.zeros((n_rows, d), x.dtype)
    return pl.pallas_call(
        _scatter_kernel,
        grid=(t // ts,),
        in_specs=[pl.BlockSpec(memory_space=pl.ANY), pl.BlockSpec((ts, d), lambda i: (i, 0)),
                  pl.BlockSpec(memory_space=pl.ANY)],
        out_specs=pl.BlockSpec(memory_space=pl.ANY),
        out_shape=jax.ShapeDtypeStruct((n_rows, d), x.dtype),
        scratch_shapes=[pltpu.SMEM((ts * TOP_K,), I32), pltpu.SemaphoreType.DMA, pltpu.SemaphoreType.DMA],
        input_output_aliases={2: 0},
        compiler_params=_cparams(("arbitrary",)),
        name="moe_scatter",
    )(dest_tiles, x, init)


def _gmm_kernel(be_ref, nu_ref, xs_ref, wgu_ref, wd_ref, ys_ref, wgub_ref, wdb_ref):
    i = pl.program_id(0)
    de = wd_ref.shape[1]
    changed = (i == 0) | (be_ref[i] != be_ref[jnp.maximum(i - 1, 0)])

    @pl.when(changed)
    def _():
        wgub_ref[...] = wgu_ref[0].astype(BF16)
        wdb_ref[...] = wd_ref[0].astype(BF16)

    @pl.when(i < nu_ref[0])
    def _():
        h = _dot(xs_ref[...].astype(BF16), wgub_ref[...])
        act = _silu(h[:, :de]) * h[:, de:]
        ys_ref[...] = _dot(act.astype(BF16), wdb_ref[...])


def _grouped_experts(block_e, n_used, xs, w_gate_up, w_down, *, blk):
    n_rows, d = xs.shape
    n_exp, _, de2 = w_gate_up.shape
    de = de2 // 2
    row = lambda i, be, nu: (jnp.minimum(i, nu[0] - 1), 0)
    grid_spec = pltpu.PrefetchScalarGridSpec(
        num_scalar_prefetch=2,
        grid=(n_rows // blk,),
        in_specs=[pl.BlockSpec((blk, d), row),
                  pl.BlockSpec((1, d, de2), lambda i, be, nu: (be[i], 0, 0)),
                  pl.BlockSpec((1, de, d), lambda i, be, nu: (be[i], 0, 0))],
        out_specs=pl.BlockSpec((blk, d), row),
        scratch_shapes=[pltpu.VMEM((d, de2), BF16), pltpu.VMEM((de, d), BF16)],
    )
    return pl.pallas_call(
        _gmm_kernel,
        grid_spec=grid_spec,
        out_shape=jax.ShapeDtypeStruct((n_rows, d), F32),
        compiler_params=_cparams(("arbitrary",)),
        name="moe_experts",
    )(block_e, n_used, xs, w_gate_up, w_down)


def _combine_kernel(dest_hbm, x_ref, xb_ref, gate_ref, wsgu_ref, wsd_ref, g_ref, b_ref, ys_hbm,
                    o_ref, ob_ref, dest_smem, rows_ref, idx_sem, row_sem, *, alpha):
    tc = x_ref.shape[0]
    ds = wsd_ref.shape[0]
    idx_copy = pltpu.make_async_copy(dest_hbm.at[pl.program_id(0)], dest_smem, idx_sem)
    idx_copy.start()
    idx_copy.wait()

    def row_copy(t, k):
        r = dest_smem[t * TOP_K + k]
        return pltpu.make_async_copy(ys_hbm.at[pl.ds(r, 1), :], rows_ref.at[k, pl.ds(t, 1), :], row_sem)

    def start_body(t, carry):
        for k in range(TOP_K):
            row_copy(t, k).start()
        return carry

    def wait_body(t, carry):
        for k in range(TOP_K):
            row_copy(t, k).wait()
        return carry

    lax.fori_loop(0, tc, start_body, 0)
    h = _dot(xb_ref[...], wsgu_ref[...])
    moe = _dot((_silu(h[:, :ds]) * h[:, ds:]).astype(BF16), wsd_ref[...])
    lax.fori_loop(0, tc, wait_body, 0)
    gate = gate_ref[...]
    for k in range(TOP_K):
        moe = moe + gate[:, k:k + 1] * rows_ref[k]
    out = _layer_norm_rows(alpha * x_ref[...] + moe, g_ref[...], b_ref[...])
    o_ref[...] = out
    ob_ref[...] = out.astype(BF16)


def _combine(dest_tiles, x, xb, gate, ws_gate_up, ws_down, g, b, ys, *, alpha, tc):
    t, d = x.shape
    ds2 = ws_gate_up.shape[1]
    row = lambda i: (i, 0)
    fixed = lambda i: (0, 0)
    return pl.pallas_call(
        functools.partial(_combine_kernel, alpha=alpha),
        grid=(t // tc,),
        in_specs=[pl.BlockSpec(memory_space=pl.ANY), pl.BlockSpec((tc, d), row), pl.BlockSpec((tc, d), row),
                  pl.BlockSpec((tc, TOP_K), row), pl.BlockSpec((d, ds2), fixed), pl.BlockSpec((ds2 // 2, d), fixed),
                  pl.BlockSpec((1, d), fixed), pl.BlockSpec((1, d), fixed), pl.BlockSpec(memory_space=pl.ANY)],
        out_specs=[pl.BlockSpec((tc, d), row), pl.BlockSpec((tc, d), row)],
        out_shape=[jax.ShapeDtypeStruct((t, d), F32), jax.ShapeDtypeStruct((t, d), BF16)],
        scratch_shapes=[pltpu.SMEM((tc * TOP_K,), I32), pltpu.VMEM((TOP_K, tc, d), F32),
                        pltpu.SemaphoreType.DMA, pltpu.SemaphoreType.DMA],
        compiler_params=_cparams(("arbitrary",)),
        name="moe_combine_ln",
    )(dest_tiles, x, xb, gate, ws_gate_up, ws_down, g.reshape(1, d), b.reshape(1, d), ys)


def _moe_layer(x, xb, w_router, router_bias, w_gate_up, w_down, ws_gate_up, ws_down, g, b, *, alpha, tiles):
    t, d = x.shape
    n_exp = w_router.shape[1]
    blk, tile = tiles["moe_block"], tiles["moe_tokens"]
    top_e_t, gate_t, rank_t, counts = _route(x, w_router.T, router_bias, tm=tiles["route_tokens"])
    n_blocks = t * TOP_K // blk + n_exp
    nb = (counts[:, 0].astype(I32) + blk - 1) // blk
    blk_end = jnp.cumsum(nb)
    starts = ((blk_end - nb) * blk).astype(F32)
    n_used = blk_end[-1:]
    block_e = jnp.minimum(jnp.searchsorted(blk_end, jnp.arange(n_blocks, dtype=I32), side="right"), n_exp - 1)
    block_e = jnp.where(jnp.arange(n_blocks) < n_used[0], block_e, block_e[n_used[0] - 1]).astype(I32)
    dest_t = _destinations(top_e_t, rank_t, jnp.broadcast_to(starts[:, None], (n_exp, LANES)),
                           tm=tiles["route_tokens"])
    dest_tiles = dest_t.T.reshape(t // tile, tile * TOP_K)
    xs = _scatter_rows(dest_tiles, x, n_blocks * blk, ts=tile)
    ys = _grouped_experts(block_e, n_used.astype(I32), xs, w_gate_up, w_down, blk=blk)
    return _combine(dest_tiles, x, xb, gate_t.T, ws_gate_up.astype(BF16), ws_down.astype(BF16), g, b, ys,
                    alpha=alpha, tc=tile)


def _tile_config(t):
    return {"mm_rows": 1024, "mm_cols": 1024, "ln_rows": 256, "route_tokens": 512,
            "moe_block": 512, "moe_tokens": min(256, t)}


def _encoder(x, hg_w_in, hg_lower_bounds, hg_norm_w, hg_w_out, at_w_in, at_rel_bias, at_w_out, gd_w_in, gd_conv_w,
             gd_a_log, gd_dt_bias, gd_norm_w, gd_w_out, ln1_g, ln1_b, ln2_g, ln2_b, moe_w_router, moe_router_bias,
             moe_w_gate_up, moe_w_down, moe_ws_gate_up, moe_ws_down, *, tiles):
    batch, seq, d = x.shape
    t = batch * seq
    depth = ln1_g.shape[0]
    alpha = (2.0 * depth) ** 0.25
    tm, tn = tiles["mm_rows"], tiles["mm_cols"]
    xf = x.reshape(t, d)
    xb = xf.astype(BF16)
    for i in range(depth):
        j, kind = divmod(i, 3)
        if kind == 0:
            proj = _matmul(xb, hg_w_in[j].astype(BF16), tm=tm, tn=tn)
            h = _hgrn_core(proj, hg_lower_bounds, hg_norm_w[j], layer_j=j, batch=batch, seq=seq, d_model=d)
            w_out = hg_w_out[j]
        elif kind == 1:
            proj = _matmul(xb, at_w_in[j].astype(BF16), tm=tm, tn=tn)
            qi = jnp.arange(CHUNK)[:, None]
            kj = jnp.arange((AT_LEFT_CHUNKS + 1) * CHUNK)[None, :]
            rel_idx = jnp.clip(qi + AT_LEFT_CHUNKS * CHUNK - kj, -(CHUNK - 1), AT_MAX_REL) + (CHUNK - 1)
            bias = at_rel_bias[j].astype(F32)[:, rel_idx]
            h = _attn_core(proj, bias, batch=batch, seq=seq, d_model=d)
            w_out = at_w_out[j]
        else:
            hd = gd_norm_w.shape[1]
            v_heads = gd_a_log.shape[1]
            k_heads = d // hd
            main = 2 * k_heads * hd + 2 * v_heads * hd
            w_in = gd_w_in[j]
            qkvz = _matmul(xb, w_in[:, :main].astype(BF16), tm=tm, tn=tn)
            w_ba = jnp.pad(w_in[:, main:], ((0, 0), (0, LANES - 2 * v_heads))).astype(BF16)
            ba = _matmul(xb, w_ba, tm=tm, tn=LANES)
            ba_t = ba.reshape(batch, seq // CHUNK, CHUNK, LANES).transpose(0, 3, 1, 2)
            head_params = jnp.zeros((v_heads, 8, LANES), F32)
            head_params = head_params.at[:, 0, :].set(gd_a_log[j].astype(F32)[:, None])
            head_params = head_params.at[:, 1, :].set(gd_dt_bias[j].astype(F32)[:, None])
            h = _gdn_core(qkvz, ba, ba_t, gd_conv_w[j], head_params, gd_norm_w[j],
                          batch=batch, seq=seq, k_heads=k_heads, v_heads=v_heads)
            w_out = gd_w_out[j]
        xf, xb = _matmul_residual_ln(h, w_out.astype(BF16), xf, ln1_g[i], ln1_b[i], alpha=alpha,
                                     tm=tiles["ln_rows"])
        xf, xb = _moe_layer(xf, xb, moe_w_router[i], moe_router_bias[i], moe_w_gate_up[i], moe_w_down[i],
                            moe_ws_gate_up[i], moe_ws_down[i], ln2_g[i], ln2_b[i], alpha=alpha, tiles=tiles)
    return xf.reshape(batch, seq, d)


def kernel(x, hg_w_in, hg_lower_bounds, hg_norm_w, hg_w_out, at_w_in, at_rel_bias, at_w_out, gd_w_in, gd_conv_w,
           gd_a_log, gd_dt_bias, gd_norm_w, gd_w_out, ln1_g, ln1_b, ln2_g, ln2_b, moe_w_router, moe_router_bias,
           moe_w_gate_up, moe_w_down, moe_ws_gate_up, moe_ws_down):
    tiles = _tile_config(x.shape[0] * x.shape[1])
    return _encoder(x, hg_w_in, hg_lower_bounds, hg_norm_w, hg_w_out, at_w_in, at_rel_bias, at_w_out, gd_w_in,
                    gd_conv_w, gd_a_log, gd_dt_bias, gd_norm_w, gd_w_out, ln1_g, ln1_b, ln2_g, ln2_b, moe_w_router,
                    moe_router_bias, moe_w_gate_up, moe_w_down, moe_ws_gate_up, moe_ws_down, tiles=tiles)
```

```python
import functools

import jax
import jax.numpy as jnp
from jax import lax
from jax.experimental import pallas as pl
from jax.experimental.pallas import tpu as pltpu

F32 = jnp.float32
BF16 = jnp.bfloat16
I32 = jnp.int32

CHUNK = 64
LANES = 128
LN_EPS = 1e-5
RMS_EPS = 1e-6
AT_HEAD_DIM = 64
AT_LEFT_CHUNKS = 8
AT_MAX_REL = 256
GD_CONV = 4
N_GROUPS = 8
TOPK_GROUPS = 4
TOP_K = 8
ROUTE_SCALE = 2.5
VMEM_LIMIT = 56 * 1024 * 1024


def _cparams(sem):
    return pltpu.CompilerParams(dimension_semantics=sem, vmem_limit_bytes=VMEM_LIMIT)


def _dot(a, b):
    return jnp.dot(a, b, preferred_element_type=F32)


def _dot_nt(a, b):
    return lax.dot_general(a, b, (((1,), (1,)), ((), ())), preferred_element_type=F32)


def _dot_tn(a, b):
    return lax.dot_general(a, b, (((0,), (0,)), ((), ())), preferred_element_type=F32)


def _split2(x):
    hi = x.astype(BF16)
    lo = (x - hi.astype(F32)).astype(BF16)
    return hi, lo


def _split3(x):
    hi = x.astype(BF16)
    r = x - hi.astype(F32)
    mid = r.astype(BF16)
    lo = (r - mid.astype(F32)).astype(BF16)
    return hi, mid, lo


def _dot3(a, b, dot=_dot):
    ah, al = _split2(a)
    bh, bl = _split2(b)
    return dot(ah, bh) + (dot(ah, bl) + dot(al, bh))


def _dot_exact_lhs(mask_bf16, x, dot=_dot):
    h, m, l = _split3(x)
    return dot(mask_bf16, h) + (dot(mask_bf16, m) + dot(mask_bf16, l))


def _sigmoid(x):
    return 1.0 / (1.0 + jnp.exp(-x))


def _silu(x):
    return x * _sigmoid(x)


def _softplus(x):
    return jnp.maximum(x, 0.0) + jnp.log(1.0 + jnp.exp(-jnp.abs(x)))


def _mm_kernel(x_ref, w_ref, o_ref):
    o_ref[...] = _dot(x_ref[...], w_ref[...]).astype(o_ref.dtype)


def _matmul(x, w, *, tm, tn, out_dtype=F32):
    t, k = x.shape
    n = w.shape[1]
    tm, tn = min(tm, t), min(tn, n)
    return pl.pallas_call(
        _mm_kernel,
        grid=(t // tm, n // tn),
        in_specs=[pl.BlockSpec((tm, k), lambda i, j: (i, 0)),
                  pl.BlockSpec((k, tn), lambda i, j: (0, j))],
        out_specs=pl.BlockSpec((tm, tn), lambda i, j: (i, j)),
        out_shape=jax.ShapeDtypeStruct((t, n), out_dtype),
        compiler_params=_cparams(("parallel", "arbitrary")),
        name="proj_matmul",
    )(x, w)


def _layer_norm_rows(y, g, b):
    mu = jnp.mean(y, axis=-1, keepdims=True)
    d = y - mu
    var = jnp.mean(d * d, axis=-1, keepdims=True)
    return d * lax.rsqrt(var + LN_EPS) * g + b


def _mm_ln_kernel(h_ref, w_ref, x_ref, g_ref, b_ref, o_ref, ob_ref, *, alpha):
    y = alpha * x_ref[...] + _dot(h_ref[...], w_ref[...])
    out = _layer_norm_rows(y, g_ref[...], b_ref[...])
    o_ref[...] = out
    ob_ref[...] = out.astype(BF16)


def _matmul_residual_ln(h, w, x, g, b, *, alpha, tm):
    t, k = h.shape
    d = w.shape[1]
    tm = min(tm, t)
    row = lambda i: (i, 0)
    fixed = lambda i: (0, 0)
    return pl.pallas_call(
        functools.partial(_mm_ln_kernel, alpha=alpha),
        grid=(t // tm,),
        in_specs=[pl.BlockSpec((tm, k), row), pl.BlockSpec((k, d), fixed),
                  pl.BlockSpec((tm, d), row), pl.BlockSpec((1, d), fixed), pl.BlockSpec((1, d), fixed)],
        out_specs=[pl.BlockSpec((tm, d), row), pl.BlockSpec((tm, d), row)],
        out_shape=[jax.ShapeDtypeStruct((t, d), F32), jax.ShapeDtypeStruct((t, d), BF16)],
        compiler_params=_cparams(("parallel",)),
        name="out_proj_ln",
    )(h, w, x, g.reshape(1, d), b.reshape(1, d))


def _hgrn_kernel(q_ref, f_ref, i_ref, g_ref, lbr_ref, nw_ref, o_ref, st_ref, *, layer_j, seq):
    c_len = CHUNK
    hd = q_ref.shape[1]
    lbr = lbr_ref[...]
    e = jnp.exp(lbr - jnp.max(lbr, axis=0, keepdims=True))
    soft = e / jnp.sum(e, axis=0, keepdims=True)
    lb = jnp.zeros((1, hd), F32)
    for r in range(1, layer_j + 1):
        lb = lb + soft[r:r + 1]
    nw = nw_ref[...]
    row = lax.broadcasted_iota(I32, (c_len, c_len), 0)
    col = lax.broadcasted_iota(I32, (c_len, c_len), 1)
    tril = row >= col
    tril_b = tril.astype(BF16)
    scale = hd ** -0.5
    st_ref[...] = jnp.zeros_like(st_ref)

    def body(c, carry):
        sl = pl.ds(pl.multiple_of(c * c_len, c_len), c_len)
        q = _silu(q_ref[sl, :]) * scale
        forget = lb + (1.0 - lb) * _sigmoid(f_ref[sl, :])
        logf = jnp.log(forget)
        k = 1.0 - forget
        vb = i_ref[sl, :].astype(BF16)
        gcum = _dot_exact_lhs(tril_b, logf)
        g_mid = gcum[c_len // 2 - 1:c_len // 2]
        g_last = gcum[c_len - 1:c_len]
        qg = (q * jnp.exp(gcum - g_mid)).astype(BF16)
        kg = (k * jnp.exp(g_mid - gcum)).astype(BF16)
        att = jnp.where(tril, _dot_nt(qg, kg), 0.0).astype(BF16)
        st = st_ref[...]
        o = _dot(att, vb) + _dot_nt((q * jnp.exp(gcum)).astype(BF16), st.astype(BF16))
        kst = (k * jnp.exp(g_last - gcum)).astype(BF16)
        st_ref[...] = st * jnp.exp(g_last) + _dot_tn(vb, kst)
        on = o * lax.rsqrt(jnp.mean(o * o, axis=-1, keepdims=True) + RMS_EPS) * nw
        o_ref[sl, :] = (on * _silu(g_ref[sl, :])).astype(o_ref.dtype)
        return carry

    lax.fori_loop(0, seq // c_len, body, 0)


def _hgrn_core(proj, lb_raw, norm_w, *, layer_j, batch, seq, d_model):
    hd = norm_w.shape[0]
    heads = d_model // hd
    t = batch * seq
    blk = lambda off: pl.BlockSpec((seq, hd), lambda b, h, off=off: (b, off + h))
    return pl.pallas_call(
        functools.partial(_hgrn_kernel, layer_j=layer_j, seq=seq),
        grid=(batch, heads),
        in_specs=[blk(0), blk(heads), blk(2 * heads), blk(3 * heads),
                  pl.BlockSpec((lb_raw.shape[0], hd), lambda b, h: (0, h)),
                  pl.BlockSpec((1, hd), lambda b, h: (0, 0))],
        out_specs=pl.BlockSpec((seq, hd), lambda b, h: (b, h)),
        out_shape=jax.ShapeDtypeStruct((t, d_model), BF16),
        scratch_shapes=[pltpu.VMEM((hd, hd), F32)],
        compiler_params=_cparams(("parallel", "parallel")),
        name="hgrn_core",
    )(proj, proj, proj, proj, lb_raw, norm_w.reshape(1, hd))


def _attn_kernel(q_ref, k_ref, v_ref, bias_ref, o_ref, kp_ref, vp_ref, *, seq):
    c_len = CHUNK
    pad = AT_LEFT_CHUNKS * c_len
    band = pad + c_len
    width = q_ref.shape[1]
    heads_here = width // AT_HEAD_DIM
    zeros = jnp.zeros((pad, width), BF16)
    kp_ref[0:pad, :] = zeros
    vp_ref[0:pad, :] = zeros
    kp_ref[pad:pad + seq, :] = k_ref[...].astype(BF16)
    vp_ref[pad:pad + seq, :] = v_ref[...].astype(BF16)
    lane = lax.broadcasted_iota(I32, (c_len, width), 1)
    key_off = lax.broadcasted_iota(I32, (c_len, band), 1)
    scale = AT_HEAD_DIM ** -0.5

    def body(c, carry):
        start = pl.multiple_of(c * c_len, c_len)
        q = q_ref[pl.ds(start, c_len), :] * scale
        kb = kp_ref[pl.ds(start, band), :]
        vb = vp_ref[pl.ds(start, band), :]
        valid = key_off + (c - AT_LEFT_CHUNKS) * c_len >= 0
        out = jnp.zeros((c_len, width), F32)
        for h in range(heads_here):
            in_head = (lane >= h * AT_HEAD_DIM) & (lane < (h + 1) * AT_HEAD_DIM)
            qh = jnp.where(in_head, q, 0.0).astype(BF16)
            s = _dot_nt(qh, kb) + bias_ref[h]
            s = jnp.where(valid, s, -jnp.inf)
            p = jnp.exp(s - jnp.max(s, axis=-1, keepdims=True))
            oh = _dot(p.astype(BF16), vb) / jnp.sum(p, axis=-1, keepdims=True)
            out = jnp.where(in_head, oh, out)
        o_ref[pl.ds(start, c_len), :] = out.astype(o_ref.dtype)
        return carry

    lax.fori_loop(0, seq // c_len, body, 0)


def _attn_core(qkv, bias, *, batch, seq, d_model):
    t = batch * seq
    nblk = d_model // LANES
    heads_here = LANES // AT_HEAD_DIM
    band = (AT_LEFT_CHUNKS + 1) * CHUNK
    blk = lambda off: pl.BlockSpec((seq, LANES), lambda b, h, off=off: (b, off + h))
    return pl.pallas_call(
        functools.partial(_attn_kernel, seq=seq),
        grid=(batch, nblk),
        in_specs=[blk(0), blk(nblk), blk(2 * nblk),
                  pl.BlockSpec((heads_here, CHUNK, band), lambda b, h: (h, 0, 0))],
        out_specs=pl.BlockSpec((seq, LANES), lambda b, h: (b, h)),
        out_shape=jax.ShapeDtypeStruct((t, d_model), BF16),
        scratch_shapes=[pltpu.VMEM((seq + AT_LEFT_CHUNKS * CHUNK, LANES), BF16),
                        pltpu.VMEM((seq + AT_LEFT_CHUNKS * CHUNK, LANES), BF16)],
        compiler_params=_cparams(("parallel", "parallel")),
        name="attn_core",
    )(qkv, qkv, qkv, bias)


def _unit_lower_inverse(lm):
    n = lm.shape[0]
    row = lax.broadcasted_iota(I32, (n, n), 0)
    col = lax.broadcasted_iota(I32, (n, n), 1)
    x = -lm
    inv = jnp.where(row == col, 1.0, 0.0) + x
    p = x
    power = 2
    while power < n:
        pb = p.astype(BF16)
        p = _dot(pb, pb)
        inv = inv + _dot(inv.astype(BF16), p.astype(BF16))
        power *= 2
    return inv


def _gdn_kernel(q_ref, k_ref, v_ref, z_ref, cwq_ref, cwk_ref, cwv_ref, ba_ref, aT_ref, hp_ref, nw_ref,
                o_ref, qn_ref, kn_ref, vc_ref, u_ref, w_ref, att_ref, qi_ref, ks_ref, dl_ref, st_ref,
                *, seq, rep, v_heads, unroll_a):
    c_len = CHUNK
    hd = q_ref.shape[1]
    n_chunks = seq // c_len
    hk = pl.program_id(1)

    def conv_silu(x, w):
        rows = lax.broadcasted_iota(I32, x.shape, 0)
        y = x * w[GD_CONV - 1:GD_CONV]
        for s in range(1, GD_CONV):
            shifted = jnp.where(rows >= s, pltpu.roll(x, s, axis=0), 0.0)
            y = y + shifted * w[GD_CONV - 1 - s:GD_CONV - s]
        return _silu(y)

    q = conv_silu(q_ref[...], cwq_ref[...])
    qn_ref[...] = q * lax.rsqrt(jnp.sum(q * q, axis=-1, keepdims=True) + RMS_EPS) * (hd ** -0.5)
    k = conv_silu(k_ref[...], cwk_ref[...])
    kn_ref[...] = k * lax.rsqrt(jnp.sum(k * k, axis=-1, keepdims=True) + RMS_EPS)
    vc_ref[...] = conv_silu(v_ref[...], cwv_ref[...])

    row = lax.broadcasted_iota(I32, (c_len, c_len), 0)
    col = lax.broadcasted_iota(I32, (c_len, c_len), 1)
    lower = row >= col
    strict = row > col
    lower_b = lower.astype(BF16)
    upper_b = (row <= col).astype(BF16)
    lane = lax.broadcasted_iota(I32, (c_len, LANES), 1)
    nw = nw_ref[...]

    def prep_chunk(c):
        sl = pl.ds(pl.multiple_of(c * c_len, c_len), c_len)
        kc = kn_ref[sl, :]
        qc = qn_ref[sl, :]
        kcb = kc.astype(BF16)
        kk = _dot_nt(kcb, kcb)
        qk = _dot_nt(qc.astype(BF16), kcb)
        bac = ba_ref[sl, :]
        for j in range(rep):
            hv = rep * hk + j
            beta_col = _sigmoid(jnp.sum(jnp.where(lane == hv, bac, 0.0), axis=-1, keepdims=True))
            a_col = jnp.sum(jnp.where(lane == v_heads + hv, bac, 0.0), axis=-1, keepdims=True)
            a_log = hp_ref[j, 0:1, :]
            dt_b = hp_ref[j, 1:2, :]
            neg_a = -jnp.exp(a_log)
            g_col = neg_a * _softplus(a_col + dt_b)
            gcum_col = _dot_exact_lhs(lower_b, g_col)
            a_row = aT_ref[0, j, pl.ds(c, 1), :]
            g_row = neg_a[:, :c_len] * _softplus(a_row + dt_b[:, :c_len])
            gcum_row = _cumsum_row(g_row, upper_b)
            diff = gcum_col[:, :c_len] - gcum_row
            decay = jnp.where(lower, jnp.exp(jnp.where(lower, diff, 0.0)), 0.0)
            lm = jnp.where(strict, beta_col * kk * decay, 0.0)
            tinv = _unit_lower_inverse(lm).astype(BF16)
            vj = vc_ref[sl, j * hd:(j + 1) * hd]
            u_ref[j, sl, :] = _dot(tinv, (vj * beta_col).astype(BF16))
            w_ref[j, sl, :] = _dot(tinv, (kc * (beta_col * jnp.exp(gcum_col))).astype(BF16)).astype(BF16)
            att_ref[j, sl, :] = jnp.where(lower, qk * decay, 0.0).astype(BF16)
            qi_ref[j, sl, :] = (qc * jnp.exp(gcum_col)).astype(BF16)
            g_last = gcum_col[c_len - 1:c_len]
            ks_ref[j, sl, :] = (kc * jnp.exp(g_last - gcum_col)).astype(BF16)
            dl_ref[j, pl.ds(c, 1), :] = jnp.exp(g_last)

    def prep_body(cc, carry):
        for i in range(unroll_a):
            prep_chunk(cc * unroll_a + i)
        return carry

    lax.fori_loop(0, n_chunks // unroll_a, prep_body, 0)

    st_ref[...] = jnp.zeros_like(st_ref)

    def scan_body(c, carry):
        sl = pl.ds(pl.multiple_of(c * c_len, c_len), c_len)
        for j in range(rep):
            st = st_ref[j]
            stb = st.astype(BF16)
            v_new = u_ref[j, sl, :] - _dot(w_ref[j, sl, :], stb)
            v_newb = v_new.astype(BF16)
            o = _dot(qi_ref[j, sl, :], stb) + _dot(att_ref[j, sl, :], v_newb)
            st_ref[j] = dl_ref[j, pl.ds(c, 1), :] * st + _dot_tn(ks_ref[j, sl, :], v_newb)
            on = o * lax.rsqrt(jnp.mean(o * o, axis=-1, keepdims=True) + RMS_EPS) * nw
            o_ref[sl, j * hd:(j + 1) * hd] = (on * _silu(z_ref[sl, j * hd:(j + 1) * hd])).astype(o_ref.dtype)
        return carry

    lax.fori_loop(0, n_chunks, scan_body, 0)


def _cumsum_row(g_row, upper_b):
    c_len = g_row.shape[1]
    gb = jnp.broadcast_to(g_row, (8, c_len))
    h, m, l = _split3(gb)
    return (_dot(h, upper_b) + (_dot(m, upper_b) + _dot(l, upper_b)))[0:1]


def _gdn_core(qkvz, ba, ba_t, conv_w, head_params, norm_w, *, batch, seq, k_heads, v_heads):
    hd = norm_w.shape[0]
    rep = v_heads // k_heads
    t = batch * seq
    n_chunks = seq // CHUNK
    vw = rep * hd
    v_off = 2 * k_heads // rep
    return pl.pallas_call(
        functools.partial(_gdn_kernel, seq=seq, rep=rep, v_heads=v_heads, unroll_a=2),
        grid=(batch, k_heads),
        in_specs=[pl.BlockSpec((seq, hd), lambda b, h: (b, h)),
                  pl.BlockSpec((seq, hd), lambda b, h: (b, k_heads + h)),
                  pl.BlockSpec((seq, vw), lambda b, h: (b, v_off + h)),
                  pl.BlockSpec((seq, vw), lambda b, h: (b, v_off + k_heads + h)),
                  pl.BlockSpec((GD_CONV, hd), lambda b, h: (0, h)),
                  pl.BlockSpec((GD_CONV, hd), lambda b, h: (0, k_heads + h)),
                  pl.BlockSpec((GD_CONV, vw), lambda b, h: (0, v_off + h)),
                  pl.BlockSpec((seq, LANES), lambda b, h: (b, 0)),
                  pl.BlockSpec((1, rep, n_chunks, CHUNK), lambda b, h: (b, k_heads + h, 0, 0)),
                  pl.BlockSpec((rep, 8, LANES), lambda b, h: (h, 0, 0)),
                  pl.BlockSpec((1, hd), lambda b, h: (0, 0))],
        out_specs=pl.BlockSpec((seq, vw), lambda b, h: (b, h)),
        out_shape=jax.ShapeDtypeStruct((t, v_heads * hd), BF16),
        scratch_shapes=[pltpu.VMEM((seq, hd), F32), pltpu.VMEM((seq, hd), F32), pltpu.VMEM((seq, vw), F32),
                        pltpu.VMEM((rep, seq, hd), F32), pltpu.VMEM((rep, seq, hd), BF16),
                        pltpu.VMEM((rep, seq, CHUNK), BF16), pltpu.VMEM((rep, seq, hd), BF16),
                        pltpu.VMEM((rep, seq, hd), BF16), pltpu.VMEM((rep, n_chunks, LANES), F32),
                        pltpu.VMEM((rep, hd, hd), F32)],
        compiler_params=_cparams(("parallel", "parallel")),
        name="gdn_core",
    )(qkvz, qkvz, qkvz, qkvz, conv_w, conv_w, conv_w, ba, ba_t, head_params, norm_w.reshape(1, hd))


def _first_index_of_max(vals, idx, n):
    m = jnp.max(vals, axis=0, keepdims=True)
    first = jnp.min(jnp.where(vals == m, idx, n), axis=0, keepdims=True)
    return m, first


def _router_kernel(x_ref, wt_ref, rb_ref, te_ref, gt_ref, rk_ref, cnt_ref, carry_ref):
    n_exp = wt_ref.shape[0]
    tm = x_ref.shape[0]
    gsz = n_exp // N_GROUPS
    neg_inf = -jnp.inf

    @pl.when(pl.program_id(0) == 0)
    def _():
        carry_ref[...] = jnp.zeros_like(carry_ref)

    logits = _dot3(wt_ref[...], x_ref[...], dot=_dot_nt)
    scores = _sigmoid(logits)
    sel = scores + rb_ref[:, 0:1]

    gidx = lax.broadcasted_iota(I32, (gsz, tm), 0)
    grp_rows = []
    for g in range(N_GROUPS):
        blk = sel[g * gsz:(g + 1) * gsz]
        m1, first = _first_index_of_max(blk, gidx, gsz)
        m2 = jnp.max(jnp.where(gidx == first, neg_inf, blk), axis=0, keepdims=True)
        grp_rows.append(m1 + m2)
    grp = jnp.concatenate(grp_rows, axis=0)
    gi = lax.broadcasted_iota(I32, (N_GROUPS, tm), 0)
    keep = jnp.zeros((N_GROUPS, tm), F32)
    for _ in range(TOPK_GROUPS):
        _, first = _first_index_of_max(grp, gi, N_GROUPS)
        hit = gi == first
        keep = jnp.where(hit, 1.0, keep)
        grp = jnp.where(hit, neg_inf, grp)
    cur = jnp.concatenate(
        [jnp.where(keep[g:g + 1] > 0.0, sel[g * gsz:(g + 1) * gsz], neg_inf) for g in range(N_GROUPS)], axis=0)

    ei = lax.broadcasted_iota(I32, (n_exp, tm), 0)
    chosen = jnp.zeros((n_exp, tm), F32)
    top_e, gates = [], []
    for _ in range(TOP_K):
        _, first = _first_index_of_max(cur, ei, n_exp)
        hit = ei == first
        top_e.append(first)
        gates.append(jnp.sum(jnp.where(hit, scores, 0.0), axis=0, keepdims=True))
        chosen = jnp.where(hit, 1.0, chosen)
        cur = jnp.where(hit, neg_inf, cur)
    gate = jnp.concatenate(gates, axis=0)
    gate = gate / jnp.sum(gate, axis=0, keepdims=True) * ROUTE_SCALE

    r = lax.broadcasted_iota(I32, (tm, tm), 0)
    c = lax.broadcasted_iota(I32, (tm, tm), 1)
    incl = _dot(chosen.astype(BF16), (r <= c).astype(BF16))
    before = carry_ref[:, 0:1] + incl - chosen
    ranks = [jnp.sum(jnp.where(ei == top_e[k], before, 0.0), axis=0, keepdims=True) for k in range(TOP_K)]

    te_ref[...] = jnp.concatenate(top_e, axis=0)
    gt_ref[...] = gate
    rk_ref[...] = jnp.concatenate(ranks, axis=0).astype(I32)
    carry_ref[...] = carry_ref[...] + jnp.broadcast_to(incl[:, tm - 1:tm], carry_ref.shape)
    cnt_ref[...] = carry_ref[...]


def _route(x, w_router_t, router_bias, *, tm):
    t, d = x.shape
    n_exp = w_router_t.shape[0]
    tm = min(tm, t)
    tok = lambda i: (0, i)
    fixed = lambda i: (0, 0)
    rb = jnp.broadcast_to(router_bias.astype(F32)[:, None], (n_exp, LANES))
    return pl.pallas_call(
        _router_kernel,
        grid=(t // tm,),
        in_specs=[pl.BlockSpec((tm, d), lambda i: (i, 0)), pl.BlockSpec((n_exp, d), fixed),
                  pl.BlockSpec((n_exp, LANES), fixed)],
        out_specs=[pl.BlockSpec((TOP_K, tm), tok), pl.BlockSpec((TOP_K, tm), tok), pl.BlockSpec((TOP_K, tm), tok),
                   pl.BlockSpec((n_exp, LANES), fixed)],
        out_shape=[jax.ShapeDtypeStruct((TOP_K, t), I32), jax.ShapeDtypeStruct((TOP_K, t), F32),
                   jax.ShapeDtypeStruct((TOP_K, t), I32), jax.ShapeDtypeStruct((n_exp, LANES), F32)],
        scratch_shapes=[pltpu.VMEM((n_exp, LANES), F32)],
        compiler_params=_cparams(("arbitrary",)),
        name="moe_route",
    )(x, w_router_t, rb)


def _dest_kernel(te_ref, rk_ref, st_ref, d_ref):
    n_exp = st_ref.shape[0]
    tm = te_ref.shape[1]
    ei = lax.broadcasted_iota(I32, (n_exp, tm), 0)
    start = st_ref[:, 0:1]
    rows = [jnp.sum(jnp.where(ei == te_ref[k:k + 1, :], start, 0.0), axis=0, keepdims=True) for k in range(TOP_K)]
    d_ref[...] = jnp.concatenate(rows, axis=0).astype(I32) + rk_ref[...]


def _destinations(top_e_t, rank_t, starts, *, tm):
    t = top_e_t.shape[1]
    n_exp = starts.shape[0]
    tm = min(tm, t)
    tok = lambda i: (0, i)
    return pl.pallas_call(
        _dest_kernel,
        grid=(t // tm,),
        in_specs=[pl.BlockSpec((TOP_K, tm), tok), pl.BlockSpec((TOP_K, tm), tok),
                  pl.BlockSpec((n_exp, LANES), lambda i: (0, 0))],
        out_specs=pl.BlockSpec((TOP_K, tm), tok),
        out_shape=jax.ShapeDtypeStruct((TOP_K, t), I32),
        compiler_params=_cparams(("parallel",)),
        name="moe_dest",
    )(top_e_t, rank_t, starts)


def _scatter_kernel(dest_hbm, x_ref, init_hbm, xs_hbm, dest_smem, idx_sem, row_sem):
    del init_hbm
    ts = x_ref.shape[0]
    idx_copy = pltpu.make_async_copy(dest_hbm.at[pl.program_id(0)], dest_smem, idx_sem)
    idx_copy.start()
    idx_copy.wait()

    def row_copy(t, k):
        r = dest_smem[t * TOP_K + k]
        return pltpu.make_async_copy(x_ref.at[pl.ds(t, 1), :], xs_hbm.at[pl.ds(r, 1), :], row_sem)

    def start_body(t, carry):
        for k in range(TOP_K):
            row_copy(t, k).start()
        return carry

    def wait_body(t, carry):
        for k in range(TOP_K):
            row_copy(t, k).wait()
        return carry

    lax.fori_loop(0, ts, start_body, 0)
    lax.fori_loop(0, ts, wait_body, 0)


def _scatter_rows(dest_tiles, x, n_rows, *, ts):
    t, d = x.shape
    init = jnp.zeros((n_rows, d), x.dtype)
    return pl.pallas_call(
        _scatter_kernel,
        grid=(t // ts,),
        in_specs=[pl.BlockSpec(memory_space=pl.ANY), pl.BlockSpec((ts, d), lambda i: (i, 0)),
                  pl.BlockSpec(memory_space=pl.ANY)],
        out_specs=pl.BlockSpec(memory_space=pl.ANY),
        out_shape=jax.ShapeDtypeStruct((n_rows, d), x.dtype),
        scratch_shapes=[pltpu.SMEM((ts * TOP_K,), I32), pltpu.SemaphoreType.DMA, pltpu.SemaphoreType.DMA],
        input_output_aliases={2: 0},
        compiler_params=_cparams(("arbitrary",)),
        name="moe_scatter",
    )(dest_tiles, x, init)


def _gmm_kernel(be_ref, nu_ref, xs_ref, wgu_ref, wd_ref, ys_ref, wgub_ref, wdb_ref):
    i = pl.program_id(0)
    de = wd_ref.shape[1]
    changed = (i == 0) | (be_ref[i] != be_ref[jnp.maximum(i - 1, 0)])

    @pl.when(changed)
    def _():
        wgub_ref[...] = wgu_ref[0].astype(BF16)
        wdb_ref[...] = wd_ref[0].astype(BF16)

    @pl.when(i < nu_ref[0])
    def _():
        h = _dot(xs_ref[...].astype(BF16), wgub_ref[...])
        act = _silu(h[:, :de]) * h[:, de:]
        ys_ref[...] = _dot(act.astype(BF16), wdb_ref[...])


def _grouped_experts(block_e, n_used, xs, w_gate_up, w_down, *, blk):
    n_rows, d = xs.shape
    n_exp, _, de2 = w_gate_up.shape
    de = de2 // 2
    row = lambda i, be, nu: (jnp.minimum(i, nu[0] - 1), 0)
    grid_spec = pltpu.PrefetchScalarGridSpec(
        num_scalar_prefetch=2,
        grid=(n_rows // blk,),
        in_specs=[pl.BlockSpec((blk, d), row),
                  pl.BlockSpec((1, d, de2), lambda i, be, nu: (be[i], 0, 0)),
                  pl.BlockSpec((1, de, d), lambda i, be, nu: (be[i], 0, 0))],
        out_specs=pl.BlockSpec((blk, d), row),
        scratch_shapes=[pltpu.VMEM((d, de2), BF16), pltpu.VMEM((de, d), BF16)],
    )
    return pl.pallas_call(
        _gmm_kernel,
        grid_spec=grid_spec,
        out_shape=jax.ShapeDtypeStruct((n_rows, d), F32),
        compiler_params=_cparams(("arbitrary",)),
        name="moe_experts",
    )(block_e, n_used, xs, w_gate_up, w_down)


def _combine_kernel(dest_hbm, x_ref, xb_ref, gate_ref, wsgu_ref, wsd_ref, g_ref, b_ref, ys_hbm,
                    o_ref, ob_ref, dest_smem, rows_ref, idx_sem, row_sem, *, alpha):
    tc = x_ref.shape[0]
    ds = wsd_ref.shape[0]
    idx_copy = pltpu.make_async_copy(dest_hbm.at[pl.program_id(0)], dest_smem, idx_sem)
    idx_copy.start()
    idx_copy.wait()

    def row_copy(t, k):
        r = dest_smem[t * TOP_K + k]
        return pltpu.make_async_copy(ys_hbm.at[pl.ds(r, 1), :], rows_ref.at[k, pl.ds(t, 1), :], row_sem)

    def start_body(t, carry):
        for k in range(TOP_K):
            row_copy(t, k).start()
        return carry

    def wait_body(t, carry):
        for k in range(TOP_K):
            row_copy(t, k).wait()
        return carry

    lax.fori_loop(0, tc, start_body, 0)
    h = _dot(xb_ref[...], wsgu_ref[...])
    moe = _dot((_silu(h[:, :ds]) * h[:, ds:]).astype(BF16), wsd_ref[...])
    lax.fori_loop(0, tc, wait_body, 0)
    gate = gate_ref[...]
    for k in range(TOP_K):
        moe = moe + gate[:, k:k + 1] * rows_ref[k]
    out = _layer_norm_rows(alpha * x_ref[...] + moe, g_ref[...], b_ref[...])
    o_ref[...] = out
    ob_ref[...] = out.astype(BF16)


def _combine(dest_tiles, x, xb, gate, ws_gate_up, ws_down, g, b, ys, *, alpha, tc):
    t, d = x.shape
    ds2 = ws_gate_up.shape[1]
    row = lambda i: (i, 0)
    fixed = lambda i: (0, 0)
    return pl.pallas_call(
        functools.partial(_combine_kernel, alpha=alpha),
        grid=(t // tc,),
        in_specs=[pl.BlockSpec(memory_space=pl.ANY), pl.BlockSpec((tc, d), row), pl.BlockSpec((tc, d), row),
                  pl.BlockSpec((tc, TOP_K), row), pl.BlockSpec((d, ds2), fixed), pl.BlockSpec((ds2 // 2, d), fixed),
                  pl.BlockSpec((1, d), fixed), pl.BlockSpec((1, d), fixed), pl.BlockSpec(memory_space=pl.ANY)],
        out_specs=[pl.BlockSpec((tc, d), row), pl.BlockSpec((tc, d), row)],
        out_shape=[jax.ShapeDtypeStruct((t, d), F32), jax.ShapeDtypeStruct((t, d), BF16)],
        scratch_shapes=[pltpu.SMEM((tc * TOP_K,), I32), pltpu.VMEM((TOP_K, tc, d), F32),
                        pltpu.SemaphoreType.DMA, pltpu.SemaphoreType.DMA],
        compiler_params=_cparams(("arbitrary",)),
        name="moe_combine_ln",
    )(dest_tiles, x, xb, gate, ws_gate_up, ws_down, g.reshape(1, d), b.reshape(1, d), ys)


def _moe_layer(x, xb, w_router, router_bias, w_gate_up, w_down, ws_gate_up, ws_down, g, b, *, alpha, tiles):
    t, d = x.shape
    n_exp = w_router.shape[1]
    blk, tile = tiles["moe_block"], tiles["moe_tokens"]
    top_e_t, gate_t, rank_t, counts = _route(x, w_router.T, router_bias, tm=tiles["route_tokens"])
    n_blocks = t * TOP_K // blk + n_exp
    nb = (counts[:, 0].astype(I32) + blk - 1) // blk
    blk_end = jnp.cumsum(nb)
    starts = ((blk_end - nb) * blk).astype(F32)
    n_used = blk_end[-1:]
    block_e = jnp.minimum(jnp.searchsorted(blk_end, jnp.arange(n_blocks, dtype=I32), side="right"), n_exp - 1)
    block_e = jnp.where(jnp.arange(n_blocks) < n_used[0], block_e, block_e[n_used[0] - 1]).astype(I32)
    dest_t = _destinations(top_e_t, rank_t, jnp.broadcast_to(starts[:, None], (n_exp, LANES)),
                           tm=tiles["route_tokens"])
    dest_tiles = dest_t.T.reshape(t // tile, tile * TOP_K)
    xs = _scatter_rows(dest_tiles, x, n_blocks * blk, ts=tile)
    ys = _grouped_experts(block_e, n_used.astype(I32), xs, w_gate_up, w_down, blk=blk)
    return _combine(dest_tiles, x, xb, gate_t.T, ws_gate_up.astype(BF16), ws_down.astype(BF16), g, b, ys,
                    alpha=alpha, tc=tile)


def _tile_config(t):
    return {"mm_rows": 1024, "mm_cols": 1024, "ln_rows": 256, "route_tokens": 512,
            "moe_block": 512, "moe_tokens": min(256, t)}


def _encoder(x, hg_w_in, hg_lower_bounds, hg_norm_w, hg_w_out, at_w_in, at_rel_bias, at_w_out, gd_w_in, gd_conv_w,
             gd_a_log, gd_dt_bias, gd_norm_w, gd_w_out, ln1_g, ln1_b, ln2_g, ln2_b, moe_w_router, moe_router_bias,
             moe_w_gate_up, moe_w_down, moe_ws_gate_up, moe_ws_down, *, tiles):
    batch, seq, d = x.shape
    t = batch * seq
    depth = ln1_g.shape[0]
    alpha = (2.0 * depth) ** 0.25
    tm, tn = tiles["mm_rows"], tiles["mm_cols"]
    xf = x.reshape(t, d)
    xb = xf.astype(BF16)
    for i in range(depth):
        j, kind = divmod(i, 3)
        if kind == 0:
            proj = _matmul(xb, hg_w_in[j].astype(BF16), tm=tm, tn=tn)
            h = _hgrn_core(proj, hg_lower_bounds, hg_norm_w[j], layer_j=j, batch=batch, seq=seq, d_model=d)
            w_out = hg_w_out[j]
        elif kind == 1:
            proj = _matmul(xb, at_w_in[j].astype(BF16), tm=tm, tn=tn)
            qi = jnp.arange(CHUNK)[:, None]
            kj = jnp.arange((AT_LEFT_CHUNKS + 1) * CHUNK)[None, :]
            rel_idx = jnp.clip(qi + AT_LEFT_CHUNKS * CHUNK - kj, -(CHUNK - 1), AT_MAX_REL) + (CHUNK - 1)
            bias = at_rel_bias[j].astype(F32)[:, rel_idx]
            h = _attn_core(proj, bias, batch=batch, seq=seq, d_model=d)
            w_out = at_w_out[j]
        else:
            hd = gd_norm_w.shape[1]
            v_heads = gd_a_log.shape[1]
            k_heads = d // hd
            main = 2 * k_heads * hd + 2 * v_heads * hd
            w_in = gd_w_in[j]
            qkvz = _matmul(xb, w_in[:, :main].astype(BF16), tm=tm, tn=tn)
            w_ba = jnp.pad(w_in[:, main:], ((0, 0), (0, LANES - 2 * v_heads))).astype(BF16)
            ba = _matmul(xb, w_ba, tm=tm, tn=LANES)
            ba_t = ba.reshape(batch, seq // CHUNK, CHUNK, LANES).transpose(0, 3, 1, 2)
            head_params = jnp.zeros((v_heads, 8, LANES), F32)
            head_params = head_params.at[:, 0, :].set(gd_a_log[j].astype(F32)[:, None])
            head_params = head_params.at[:, 1, :].set(gd_dt_bias[j].astype(F32)[:, None])
            h = _gdn_core(qkvz, ba, ba_t, gd_conv_w[j], head_params, gd_norm_w[j],
                          batch=batch, seq=seq, k_heads=k_heads, v_heads=v_heads)
            w_out = gd_w_out[j]
        xf, xb = _matmul_residual_ln(h, w_out.astype(BF16), xf, ln1_g[i], ln1_b[i], alpha=alpha,
                                     tm=tiles["ln_rows"])
        xf, xb = _moe_layer(xf, xb, moe_w_router[i], moe_router_bias[i], moe_w_gate_up[i], moe_w_down[i],
                            moe_ws_gate_up[i], moe_ws_down[i], ln2_g[i], ln2_b[i], alpha=alpha, tiles=tiles)
    return xf.reshape(batch, seq, d)


def kernel(x, hg_w_in, hg_lower_bounds, hg_norm_w, hg_w_out, at_w_in, at_rel_bias, at_w_out, gd_w_in, gd_conv_w,
           gd_a_log, gd_dt_bias, gd_norm_w, gd_w_out, ln1_g, ln1_b, ln2_g, ln2_b, moe_w_router, moe_router_bias,
           moe_w_gate_up, moe_w_down, moe_ws_gate_up, moe_ws_down):
    tiles = _tile_config(x.shape[0] * x.shape[1])
    return _encoder(x, hg_w_in, hg_lower_bounds, hg_norm_w, hg_w_out, at_w_in, at_rel_bias, at_w_out, gd_w_in,
                    gd_conv_w, gd_a_log, gd_dt_bias, gd_norm_w, gd_w_out, ln1_g, ln1_b, ln2_g, ln2_b, moe_w_router,
                    moe_router_bias, moe_w_gate_up, moe_w_down, moe_ws_gate_up, moe_ws_down, tiles=tiles)
```
